```python
import math, functools
import jax, jax.numpy as jnp
from jax import lax
import numpy as np

D_MODEL = 1024
BATCH = 8
SEQ = 4096
DEPTH = 1
DEC_BATCH = 128
DEC_SEQ = 1
PAST_LEN = 8192
PAGE_SIZE = 128

HEAD_DIM = 64
N_HEADS = D_MODEL // HEAD_DIM
N_KV_HEADS = 4
GROUP = N_HEADS // N_KV_HEADS
D_ATTN = N_HEADS * HEAD_DIM
D_KV = N_KV_HEADS * HEAD_DIM
D_RNN = D_MODEL
RNN_BLOCKS = 16
RNN_BLOCK = D_RNN // RNN_BLOCKS
RNN_CONV = 4
RG_C = 8.0
D_FF = ((8 * D_MODEL // 3 + 127) // 128) * 128
FFN_CONV = 3
META_TOKENS = 16
BLOCK = 128
EPS = 1e-6
SB_SCALE = 1.0 / math.sqrt(HEAD_DIM)
SB_BIAS_INIT = -8.0
D_IN = D_ATTN + 2 * D_KV + 2 * D_RNN + 2 * D_MODEL
SPLITS = (D_ATTN, D_ATTN + D_KV, D_ATTN + 2 * D_KV, D_ATTN + 2 * D_KV + D_RNN,
          D_ATTN + 2 * D_KV + 2 * D_RNN)

kernel_name = 'hybrid_stickbreak_rglru_convffn_step'


def rms_norm(x, g):
    xf = x.astype(jnp.float32)
    y = xf * lax.rsqrt(jnp.mean(xf * xf, axis=-1, keepdims=True) + EPS)
    return (y * g.astype(jnp.float32)).astype(x.dtype)


def causal_dwconv(u, buf, w, b):
    t = u.shape[1]
    ucat = jnp.concatenate([buf.astype(u.dtype), u], axis=1)
    out = b + sum(ucat[:, j:j + t] * w[j] for j in range(w.shape[0]))
    return out, ucat[:, t:]


def stick_breaking_weights(z, mask):
    log_keep = jnp.where(mask, jax.nn.log_sigmoid(-z), 0.0)
    later = jnp.concatenate([log_keep[..., 1:], jnp.zeros_like(log_keep[..., :1])], axis=-1)
    tail = lax.cumsum(later, axis=z.ndim - 1, reverse=True)
    return jnp.where(mask, jnp.exp(jax.nn.log_sigmoid(z) + tail), 0.0)


def attend_prompt(q, k, v, bias):
    b, t = q.shape[:2]
    pad = (-META_TOKENS) % BLOCK
    qp = jnp.pad(q, ((0, 0), (pad, 0), (0, 0), (0, 0), (0, 0)))
    kp = jnp.pad(k, ((0, 0), (pad, 0), (0, 0), (0, 0))).astype(jnp.float32)
    vp = jnp.pad(v, ((0, 0), (pad, 0), (0, 0), (0, 0))).astype(jnp.float32)
    tp = t + pad
    k_pos = jnp.arange(tp)
    k_valid = k_pos >= pad
    bias_f = bias.astype(jnp.float32)[None, :, :, None, None]

    def one_block(start):
        qb = lax.dynamic_slice_in_dim(qp, start, BLOCK, axis=1).astype(jnp.float32)
        q_pos = start + jnp.arange(BLOCK)
        mask = (k_pos[None, :] < q_pos[:, None]) & k_valid[None, :]
        z = jnp.einsum('bqhgd,bshd->bhgqs', qb, kp) * SB_SCALE + bias_f
        w = stick_breaking_weights(z, mask)
        return jnp.einsum('bhgqs,bshd->bqhgd', w, vp)

    out = lax.map(one_block, jnp.arange(tp // BLOCK) * BLOCK)
    out = jnp.moveaxis(out, 0, 1).reshape(b, tp, D_ATTN)
    return out[:, pad:].astype(q.dtype)


def attend_sample(q, k, v, bias, past_k, past_v):
    b, s = q.shape[:2]
    p = past_k.shape[1]
    qf = q.astype(jnp.float32) * SB_SCALE
    z = jnp.concatenate([
        jnp.einsum('bqhgd,bshd->bhgqs', qf, past_k.astype(jnp.float32)),
        jnp.einsum('bqhgd,bshd->bhgqs', qf, k.astype(jnp.float32))], axis=-1)
    z = z + bias.astype(jnp.float32)[None, :, :, None, None]
    q_pos = p + jnp.arange(s)
    k_pos = jnp.arange(p + s)
    mask = k_pos[None, :] < q_pos[:, None]
    w = stick_breaking_weights(z, mask)
    out = (jnp.einsum('bhgqs,bshd->bqhgd', w[..., :p], past_v.astype(jnp.float32))
           + jnp.einsum('bhgqs,bshd->bqhgd', w[..., p:], v.astype(jnp.float32)))
    return out.reshape(b, s, D_ATTN).astype(q.dtype)


def rg_lru(xc, h0, w_rgate, b_rgate, w_igate, b_igate, lam):
    b, t, _ = xc.shape
    xb = xc.reshape(b, t, RNN_BLOCKS, RNN_BLOCK)
    r = jax.nn.sigmoid(jnp.einsum('btnc,ncd->btnd', xb, w_rgate).reshape(b, t, D_RNN) + b_rgate).astype(jnp.float32)
    i = jax.nn.sigmoid(jnp.einsum('btnc,ncd->btnd', xb, w_igate).reshape(b, t, D_RNN) + b_igate).astype(jnp.float32)
    log_a = -RG_C * r * jax.nn.softplus(-lam.astype(jnp.float32))
    a = jnp.exp(log_a)
    u = jnp.sqrt(-jnp.expm1(2.0 * log_a)) * i * xc.astype(jnp.float32)

    def combine(lhs, rhs):
        al, bl = lhs
        ar, br = rhs
        return al * ar, ar * bl + br

    a_cum, b_cum = lax.associative_scan(combine, (a, u), axis=1)
    h = a_cum * h0.astype(jnp.float32)[:, None] + b_cum
    return h.astype(xc.dtype), h[:, -1].astype(xc.dtype)


def hybrid_layer(x, attend, rnn_h0, rnn_buf, ffn_buf, lw):
    (g_mix, w_in, sb_bias, w_rnn_conv, b_rnn_conv, w_rgate, b_rgate, w_igate, b_igate,
     rglru_lambda, w_out, g_ffn, w_up, w_ffn_conv, b_ffn_conv, w_down) = lw
    b, t, _ = x.shape
    n = rms_norm(x, g_mix)
    q, k, v, xr, gr, gates = jnp.split(n @ w_in, SPLITS, axis=-1)
    q = q.reshape(b, t, N_KV_HEADS, GROUP, HEAD_DIM)
    k = k.reshape(b, t, N_KV_HEADS, HEAD_DIM)
    v = v.reshape(b, t, N_KV_HEADS, HEAD_DIM)
    attn = attend(q, k, v, sb_bias.reshape(N_KV_HEADS, GROUP))
    xc, rnn_buf_new = causal_dwconv(xr, rnn_buf, w_rnn_conv, b_rnn_conv)
    hseq, h_last = rg_lru(xc, rnn_h0, w_rgate, b_rgate, w_igate, b_igate, rglru_lambda)
    rnn = hseq * jax.nn.gelu(gr)
    gate_a, gate_b = jnp.split(jax.nn.sigmoid(gates), 2, axis=-1)
    x = x + (gate_a * attn + gate_b * rnn) @ w_out
    up, ffn_buf_new = causal_dwconv(rms_norm(x, g_ffn) @ w_up, ffn_buf, w_ffn_conv, b_ffn_conv)
    ua, ub = jnp.split(up, 2, axis=-1)
    x = x + (jax.nn.gelu(ua) * ub) @ w_down
    return x, k, v, h_last, rnn_buf_new, ffn_buf_new


def setup_inputs(seed: int = 0) -> dict:
    key = jax.random.key(seed)
    ks = jax.random.split(key, 28)

    def nrm(k, shape, scale):
        return jax.random.normal(k, shape, jnp.float32) * scale

    n_pages = PAST_LEN // PAGE_SIZE
    n_phys = (DEC_BATCH * n_pages * 5) // 4
    page_table = jax.random.permutation(ks[4], n_phys)[:DEC_BATCH * n_pages].reshape(DEC_BATCH, n_pages).astype(jnp.int32)
    u = jax.random.uniform(ks[17], (DEPTH, D_RNN), jnp.float32, minval=0.9, maxval=0.999)
    a_base = u ** (1.0 / RG_C)
    lam = jnp.log(a_base) - jnp.log1p(-a_base)
    return {
        'x_prompt': nrm(ks[0], (BATCH, SEQ, D_MODEL), 1.0),
        'x_sample': nrm(ks[1], (DEC_BATCH, DEC_SEQ, D_MODEL), 1.0),
        'cache_k': nrm(ks[2], (DEPTH, n_phys, PAGE_SIZE, N_KV_HEADS, HEAD_DIM), 1.0),
        'cache_v': nrm(ks[3], (DEPTH, n_phys, PAGE_SIZE, N_KV_HEADS, HEAD_DIM), 1.0),
        'page_table': page_table,
        'state_rnn_h': nrm(ks[5], (DEPTH, DEC_BATCH, D_RNN), 0.5),
        'state_rnn_conv': nrm(ks[6], (DEPTH, DEC_BATCH, RNN_CONV - 1, D_RNN), 1.0),
        'state_ffn_conv': nrm(ks[7], (DEPTH, DEC_BATCH, FFN_CONV - 1, 2 * D_FF), 1.0),
        'meta_tokens': nrm(ks[8], (META_TOKENS, D_MODEL), 1.0),
        'g_mix': 1.0 + nrm(ks[9], (DEPTH, D_MODEL), 0.02),
        'w_in': nrm(ks[10], (DEPTH, D_MODEL, D_IN), D_MODEL ** -0.5),
        'sb_bias': SB_BIAS_INIT + nrm(ks[25], (DEPTH, N_HEADS), 0.1),
        'w_rnn_conv': nrm(ks[11], (DEPTH, RNN_CONV, D_RNN), RNN_CONV ** -0.5),
        'b_rnn_conv': nrm(ks[12], (DEPTH, D_RNN), 0.01),
        'w_rgate': nrm(ks[13], (DEPTH, RNN_BLOCKS, RNN_BLOCK, RNN_BLOCK), RNN_BLOCK ** -0.5),
        'b_rgate': nrm(ks[14], (DEPTH, D_RNN), 0.01),
        'w_igate': nrm(ks[15], (DEPTH, RNN_BLOCKS, RNN_BLOCK, RNN_BLOCK), RNN_BLOCK ** -0.5),
        'b_igate': nrm(ks[16], (DEPTH, D_RNN), 0.01),
        'rglru_lambda': lam,
        'w_out': nrm(ks[18], (DEPTH, D_MODEL, D_MODEL), D_MODEL ** -0.5),
        'g_ffn': 1.0 + nrm(ks[19], (DEPTH, D_MODEL), 0.02),
        'w_up': nrm(ks[20], (DEPTH, D_MODEL, 2 * D_FF), D_MODEL ** -0.5),
        'w_ffn_conv': nrm(ks[21], (DEPTH, FFN_CONV, 2 * D_FF), FFN_CONV ** -0.5),
        'b_ffn_conv': nrm(ks[22], (DEPTH, 2 * D_FF), 0.01),
        'w_down': nrm(ks[23], (DEPTH, D_FF, D_MODEL), D_FF ** -0.5),
        'g_final': 1.0 + nrm(ks[24], (D_MODEL,), 0.02),
    }


def reference(x_prompt, x_sample, cache_k, cache_v, page_table, state_rnn_h, state_rnn_conv,
              state_ffn_conv, meta_tokens, g_mix, w_in, sb_bias, w_rnn_conv, b_rnn_conv, w_rgate,
              b_rgate, w_igate, b_igate, rglru_lambda, w_out, g_ffn, w_up, w_ffn_conv, b_ffn_conv,
              w_down, g_final):
    dt = x_prompt.dtype
    b = x_prompt.shape[0]
    db = x_sample.shape[0]
    per_layer = (g_mix, w_in, sb_bias, w_rnn_conv, b_rnn_conv, w_rgate, b_rgate, w_igate, b_igate,
                 rglru_lambda, w_out, g_ffn, w_up, w_ffn_conv, b_ffn_conv, w_down)

    xp = jnp.concatenate([jnp.broadcast_to(meta_tokens[None].astype(dt), (b, META_TOKENS, D_MODEL)), x_prompt], axis=1)
    pk, pv, ph, prc, pfc = [], [], [], [], []
    for layer in range(DEPTH):
        lw = tuple(p[layer] for p in per_layer)
        xp, k, v, h, rb, fb = hybrid_layer(
            xp, attend_prompt, jnp.zeros((b, D_RNN), dt),
            jnp.zeros((b, RNN_CONV - 1, D_RNN), dt), jnp.zeros((b, FFN_CONV - 1, 2 * D_FF), dt), lw)
        pk.append(k); pv.append(v); ph.append(h); prc.append(rb); pfc.append(fb)
    y_prompt = rms_norm(xp, g_final)[:, META_TOKENS:]

    xs = x_sample
    sk, sv, sh, src, sfc = [], [], [], [], []
    for layer in range(DEPTH):
        lw = tuple(p[layer] for p in per_layer)
        past_k = cache_k[layer][page_table].reshape(db, -1, N_KV_HEADS, HEAD_DIM)
        past_v = cache_v[layer][page_table].reshape(db, -1, N_KV_HEADS, HEAD_DIM)
        attend = functools.partial(attend_sample, past_k=past_k, past_v=past_v)
        xs, k, v, h, rb, fb = hybrid_layer(
            xs, attend, state_rnn_h[layer], state_rnn_conv[layer], state_ffn_conv[layer], lw)
        sk.append(k); sv.append(v); sh.append(h); src.append(rb); sfc.append(fb)
    y_sample = rms_norm(xs, g_final)

    return (y_prompt, y_sample,
            jnp.stack(pk), jnp.stack(pv), jnp.stack(ph), jnp.stack(prc), jnp.stack(pfc),
            jnp.stack(sk), jnp.stack(sv), jnp.stack(sh), jnp.stack(src), jnp.stack(sfc))
```

```python
import functools
import math

import numpy as np
import jax
import jax.numpy as jnp
from jax import lax
from jax.experimental import pallas as pl
from jax.experimental.pallas import tpu as pltpu

D_MODEL = 1024
HEAD_DIM = 64
N_HEADS = 16
N_KV_HEADS = 4
GROUP = 4
D_KV = N_KV_HEADS * HEAD_DIM
D_RNN = 1024
RNN_BLOCKS = 16
RNN_BLOCK = 64
RNN_CONV = 4
RG_C = 8.0
D_FF = 2816
FFN_CONV = 3
META_TOKENS = 16
EPS = 1e-6
SB_SCALE = 1.0 / math.sqrt(HEAD_DIM)
PAGE_SIZE = 128

LANES = 128
SUBLANES = 8
MXU_DIM = 256
META_BLOCK = 128
VMEM_LIMIT = 56 * 1024 * 1024

C_Q, C_K, C_V, C_XR, C_GR, C_GA, C_GB, C_END = (
    0, 1024, 1280, 1536, 2560, 3584, 4608, 5632)

_PERM = np.arange(D_MODEL).reshape(N_KV_HEADS, GROUP, HEAD_DIM).transpose(1, 0, 2).reshape(-1)
_INV_PERM = np.argsort(_PERM)
_BLOCK_PERM = np.arange(RNN_BLOCKS).reshape(N_KV_HEADS, GROUP).T.reshape(-1)


def _bf16(x):
    return x.astype(jnp.bfloat16)


def _dot(a, b):
    return jnp.dot(a, b, preferred_element_type=jnp.float32)


def _rms_norm(x, g):
    ms = jnp.mean(x * x, axis=-1, keepdims=True)
    return x * lax.rsqrt(ms + EPS) * g


def _sigmoid(x):
    return 1.0 / (1.0 + jnp.exp(-x))


def _gelu_tanh(x):
    c = math.sqrt(2.0 / math.pi)
    return 0.5 * x * (1.0 + jnp.tanh(c * (x + 0.044715 * (x * x * x))))


def _softplus(x):
    return jnp.maximum(x, 0.0) + jnp.log1p(jnp.exp(-jnp.abs(x)))


def _log_sigmoid_pair(z):
    l = jnp.log1p(jnp.exp(-jnp.abs(z)))
    return jnp.minimum(z, 0.0) - l, -jnp.maximum(z, 0.0) - l


def _rglru_gates(xc, wg_ref, b_r, b_i, lam):
    xcb = _bf16(xc)
    r_parts, i_parts = [], []
    for m in range(D_RNN // MXU_DIM):
        gm = _dot(xcb[:, m * MXU_DIM:(m + 1) * MXU_DIM], wg_ref[m])
        r_parts.append(gm[:, :MXU_DIM])
        i_parts.append(gm[:, MXU_DIM:])
    r = _sigmoid(jnp.concatenate(r_parts, axis=-1) + b_r)
    ig = _sigmoid(jnp.concatenate(i_parts, axis=-1) + b_i)
    log_a = (-RG_C) * r * _softplus(-lam)
    a = jnp.exp(log_a)
    u = jnp.sqrt(-jnp.tanh(log_a) * (a * a + 1.0)) * ig * xc
    return a, u


def _prompt_in_kernel(x_ref, gmix_ref, win_ref, cw_ref, cb_ref, wg_ref, br_ref, bi_ref,
                      lam_ref, h0_ref, conv0_ref,
                      q_ref, k_ref, v_ref, kt_ref, vb_ref, ga_ref, mr_ref, hlast_ref, tail_ref,
                      h_carry, xr_buf, a_s, u_s, h_s, *, rows, tk, first_valid):
    i = pl.program_id(1)

    @pl.when(i == 0)
    def _():
        h_carry[...] = h0_ref[...]
        xr_buf[0:SUBLANES, :] = conv0_ref[...]

    n = _bf16(_rms_norm(x_ref[...], gmix_ref[...]))

    q_ref[...] = _bf16(_dot(n, win_ref[:, C_Q:C_K]) * SB_SCALE)

    k = _dot(n, win_ref[:, C_K:C_V])
    k_ref[...] = k
    kt = _bf16(k.T)
    v = _dot(n, win_ref[:, C_V:C_XR])
    v_ref[...] = v
    for t in range(rows // tk):
        vb_ref[t] = _bf16(v[t * tk:(t + 1) * tk, :])
        for h in range(N_KV_HEADS):
            kt_ref[h, t] = kt[h * HEAD_DIM:(h + 1) * HEAD_DIM, t * tk:(t + 1) * tk]

    xr = _dot(n, win_ref[:, C_XR:C_GR])
    xr_buf[SUBLANES:SUBLANES + rows, :] = xr
    xc = cb_ref[...] + cw_ref[3:4, :] * xr
    for j in range(RNN_CONV - 1):
        off = SUBLANES - (RNN_CONV - 1) + j
        xc = xc + cw_ref[j:j + 1, :] * xr_buf[off:off + rows, :]

    a, u = _rglru_gates(xc, wg_ref, br_ref[...], bi_ref[...], lam_ref[...])
    if first_valid > 0:
        row = lax.broadcasted_iota(jnp.int32, u.shape, 0)
        u = jnp.where(row >= first_valid, u, 0.0)
    a_s[...] = a
    u_s[...] = u

    sub = lax.broadcasted_iota(jnp.int32, (SUBLANES, D_RNN), 0)

    def group(gi, hprev):
        r0 = pl.multiple_of(gi * SUBLANES, SUBLANES)
        ag = a_s[pl.ds(r0, SUBLANES), :]
        ug = u_s[pl.ds(r0, SUBLANES), :]
        for d in (1, 2, 4):
            a_sh = jnp.where(sub >= d, pltpu.roll(ag, d, axis=0), 1.0)
            u_sh = jnp.where(sub >= d, pltpu.roll(ug, d, axis=0), 0.0)
            ug = ag * u_sh + ug
            ag = ag * a_sh
        hg = ag * hprev + ug
        h_s[pl.ds(r0, SUBLANES), :] = hg
        return hg[SUBLANES - 1:SUBLANES, :]

    h_end = lax.fori_loop(0, rows // SUBLANES, group, h_carry[...], unroll=2)
    h_carry[...] = h_end
    hlast_ref[...] = h_end

    gr = _dot(n, win_ref[:, C_GR:C_GA])
    rnn = h_s[...] * _gelu_tanh(gr)
    ga_ref[...] = _sigmoid(_dot(n, win_ref[:, C_GA:C_GB]))
    mr_ref[...] = _sigmoid(_dot(n, win_ref[:, C_GB:C_END])) * rnn

    last = xr_buf[rows:rows + SUBLANES, :]
    tail_ref[...] = last
    xr_buf[0:SUBLANES, :] = last


def _whole_spec(shape):
    nd = len(shape)
    return pl.BlockSpec(shape, lambda *_: (0,) * nd)


def _const_spec(shape):
    nd = len(shape)
    return pl.BlockSpec(shape, lambda *_: (0,) * nd, pipeline_mode=pl.Buffered(1))


def _prompt_in(x, pw, h0, conv0, *, rows, tk, first_valid):
    b, t, _ = x.shape
    nt = t // tk
    kern = functools.partial(_prompt_in_kernel, rows=rows, tk=tk, first_valid=first_valid)
    row_spec = lambda w: pl.BlockSpec((None, rows, w), lambda bi, i: (bi, i, 0))
    out_shape = (
        jax.ShapeDtypeStruct((b, t, D_MODEL), jnp.bfloat16),
        jax.ShapeDtypeStruct((b, t, D_KV), jnp.float32),
        jax.ShapeDtypeStruct((b, t, D_KV), jnp.float32),
        jax.ShapeDtypeStruct((b, N_KV_HEADS, nt, HEAD_DIM, tk), jnp.bfloat16),
        jax.ShapeDtypeStruct((b, nt, tk, D_KV), jnp.bfloat16),
        jax.ShapeDtypeStruct((b, t, D_MODEL), jnp.float32),
        jax.ShapeDtypeStruct((b, t, D_MODEL), jnp.float32),
        jax.ShapeDtypeStruct((b, 1, D_RNN), jnp.float32),
        jax.ShapeDtypeStruct((b, SUBLANES, D_RNN), jnp.float32),
    )
    out_specs = (
        row_spec(D_MODEL), row_spec(D_KV), row_spec(D_KV),
        pl.BlockSpec((None, N_KV_HEADS, rows // tk, HEAD_DIM, tk), lambda bi, i: (bi, 0, i, 0, 0)),
        pl.BlockSpec((None, rows // tk, tk, D_KV), lambda bi, i: (bi, i, 0, 0)),
        row_spec(D_MODEL), row_spec(D_MODEL),
        pl.BlockSpec((None, 1, D_RNN), lambda bi, i: (bi, 0, 0)),
        pl.BlockSpec((None, SUBLANES, D_RNN), lambda bi, i: (bi, 0, 0)),
    )
    in_specs = [
        row_spec(D_MODEL),
        _const_spec((1, D_MODEL)),
        _const_spec((D_MODEL, C_END)),
        _const_spec((RNN_CONV, D_RNN)),
        _const_spec((1, D_RNN)),
        _const_spec((D_RNN // MXU_DIM, MXU_DIM, 2 * MXU_DIM)),
        _const_spec((1, D_RNN)), _const_spec((1, D_RNN)), _const_spec((1, D_RNN)),
        _const_spec((1, D_RNN)),
        _const_spec((SUBLANES, D_RNN)),
    ]
    return pl.pallas_call(
        kern,
        grid=(b, t // rows),
        in_specs=in_specs,
        out_specs=out_specs,
        out_shape=out_shape,
        scratch_shapes=[
            pltpu.VMEM((1, D_RNN), jnp.float32),
            pltpu.VMEM((SUBLANES + rows, D_RNN), jnp.float32),
            pltpu.VMEM((rows, D_RNN), jnp.float32),
            pltpu.VMEM((rows, D_RNN), jnp.float32),
            pltpu.VMEM((rows, D_RNN), jnp.float32),
        ],
        compiler_params=pltpu.CompilerParams(
            dimension_semantics=("arbitrary", "arbitrary"), vmem_limit_bytes=VMEM_LIMIT),
        name="prompt_in",
    )(x, pw["g_mix"], pw["w_in"], pw["w_rnn_conv"], pw["b_rnn_conv"], pw["w_gate"],
      pw["b_rgate"], pw["b_igate"], pw["lam"], h0, conv0)


def _later_matrix(n):
    jj = lax.broadcasted_iota(jnp.int32, (n, n), 0)
    ss = lax.broadcasted_iota(jnp.int32, (n, n), 1)
    return jnp.where(jj > ss, 1.0, 0.0).astype(jnp.bfloat16)


def _sb_tile(qh, bias_h, kt_tiles, v_tile, later, mask, carry, acc, tq):
    z = jnp.concatenate([_dot(qh[h], kt_tiles[h]) + bias_h[h] for h in range(N_KV_HEADS)], axis=0)
    lsz, lk = _log_sigmoid_pair(z)
    if mask is not None:
        lk = jnp.where(mask, lk, 0.0)
    tail = _dot(_bf16(lk), later) + carry
    w = jnp.exp(lsz + tail)
    if mask is not None:
        w = jnp.where(mask, w, 0.0)
    wb = _bf16(w)
    lane_head = lax.broadcasted_iota(jnp.int32, v_tile.shape, 1) // HEAD_DIM
    for h in range(N_KV_HEADS):
        vh = jnp.where(lane_head == h, v_tile, jnp.zeros_like(v_tile))
        acc = acc + _dot(wb[h * tq:(h + 1) * tq, :], vh)
    carry = carry + jnp.sum(lk, axis=-1, keepdims=True)
    return carry, acc


def _prompt_attn_kernel(bias_ref, q_ref, kt_ref, vb_ref, ktm_ref, vbm_ref, o_ref, *, tq, main):
    g = pl.program_id(1)
    i = pl.program_id(2)
    q = q_ref[...]
    qh = [q[:, h * HEAD_DIM:(h + 1) * HEAD_DIM] for h in range(N_KV_HEADS)]
    bias_h = [bias_ref[h * GROUP + g] for h in range(N_KV_HEADS)]
    carry = jnp.zeros((N_KV_HEADS * tq, 1), jnp.float32)
    acc = jnp.zeros((tq, D_KV), jnp.float32)

    tkm = vbm_ref.shape[0]
    row_m = lax.broadcasted_iota(jnp.int32, (N_KV_HEADS * tq, tkm), 0) & (tq - 1)
    col_m = lax.broadcasted_iota(jnp.int32, (N_KV_HEADS * tq, tkm), 1)
    meta_valid = col_m >= (tkm - META_TOKENS)

    if main:
        tk = vb_ref.shape[1]
        later = _later_matrix(tk)
        row = lax.broadcasted_iota(jnp.int32, (N_KV_HEADS * tq, tk), 0) & (tq - 1)
        col = lax.broadcasted_iota(jnp.int32, (N_KV_HEADS * tq, tk), 1)
        carry, acc = _sb_tile(qh, bias_h, [kt_ref[h, i] for h in range(N_KV_HEADS)], vb_ref[i],
                              later, col < row, carry, acc, tq)

        def body(jj, state):
            j = i - 1 - jj
            return _sb_tile(qh, bias_h, [kt_ref[h, j] for h in range(N_KV_HEADS)], vb_ref[j],
                            later, None, state[0], state[1], tq)

        carry, acc = lax.fori_loop(0, i, body, (carry, acc))
        mask_m = meta_valid
    else:
        mask_m = meta_valid & (col_m < row_m)

    carry, acc = _sb_tile(qh, bias_h, [ktm_ref[h, 0] for h in range(N_KV_HEADS)], vbm_ref[...],
                          _later_matrix(tkm), mask_m, carry, acc, tq)
    o_ref[...] = acc


def _prompt_attn(bias, q, kt, vb, ktm, vbm, *, tq, main):
    b, t, _ = q.shape
    nt, tk = vb.shape[1], vb.shape[2]
    tkm = vbm.shape[0]
    kern = functools.partial(_prompt_attn_kernel, tq=tq, main=main)
    return pl.pallas_call(
        kern,
        grid=(b, GROUP, t // tq),
        in_specs=[
            pl.BlockSpec(memory_space=pltpu.SMEM),
            pl.BlockSpec((None, tq, D_KV), lambda bi, g, i: (bi, i, g)),
            pl.BlockSpec((None, N_KV_HEADS, nt, HEAD_DIM, tk), lambda bi, g, i: (bi, 0, 0, 0, 0)),
            pl.BlockSpec((None, nt, tk, D_KV), lambda bi, g, i: (bi, 0, 0, 0)),
            _const_spec((N_KV_HEADS, 1, HEAD_DIM, tkm)),
            _const_spec((tkm, D_KV)),
        ],
        out_specs=pl.BlockSpec((None, tq, D_KV), lambda bi, g, i: (bi, i, g)),
        out_shape=jax.ShapeDtypeStruct((b, t, D_MODEL), jnp.float32),
        compiler_params=pltpu.CompilerParams(
            dimension_semantics=("arbitrary", "arbitrary", "arbitrary"),
            vmem_limit_bytes=VMEM_LIMIT),
        name="prompt_attn" if main else "meta_attn",
    )(bias, q, kt, vb, ktm, vbm)


FF_CHUNK = D_FF // 2


def _ffn_tail(x1, n2, up_fn, wdown_ref, gfin_ref):
    acc = jnp.zeros_like(x1)
    for c in range(D_FF // FF_CHUNK):
        a0 = c * FF_CHUNK
        b0 = D_FF + c * FF_CHUNK
        ua = up_fn(a0, a0 + FF_CHUNK)
        ub = up_fn(b0, b0 + FF_CHUNK)
        hcat = _bf16(_gelu_tanh(ua) * ub)
        acc = acc + _dot(hcat, wdown_ref[a0:a0 + FF_CHUNK, :])
    x2 = x1 + acc
    return _rms_norm(x2, gfin_ref[...])


def _prompt_out_kernel(x_ref, at_ref, ga_ref, mr_ref, wout_ref, gffn_ref, wup_ref, cw_ref, cb_ref,
                       wdown_ref, gfin_ref, tail0_ref, y_ref, tail_ref, ubuf, *, rows):
    i = pl.program_id(1)

    @pl.when(i == 0)
    def _():
        ubuf[0:SUBLANES, :] = tail0_ref[...]

    merged = _bf16(ga_ref[...] * at_ref[...] + mr_ref[...])
    x1 = x_ref[...] + _dot(merged, wout_ref[...])
    n2 = _bf16(_rms_norm(x1, gffn_ref[...]))

    def up_fn(c0, c1):
        pre = _dot(n2, wup_ref[:, c0:c1])
        ubuf[SUBLANES:SUBLANES + rows, c0:c1] = pre
        out = cb_ref[:, c0:c1] + cw_ref[2:3, c0:c1] * pre
        for j in range(FFN_CONV - 1):
            off = SUBLANES - (FFN_CONV - 1) + j
            out = out + cw_ref[j:j + 1, c0:c1] * ubuf[off:off + rows, c0:c1]
        return out

    y_ref[...] = _ffn_tail(x1, n2, up_fn, wdown_ref, gfin_ref)
    last = ubuf[rows:rows + SUBLANES, :]
    tail_ref[...] = last
    ubuf[0:SUBLANES, :] = last


def _prompt_out(x, attn, ga, mr, pw, tail0, *, rows):
    b, t, _ = x.shape
    kern = functools.partial(_prompt_out_kernel, rows=rows)
    row_spec = pl.BlockSpec((None, rows, D_MODEL), lambda bi, i: (bi, i, 0))
    return pl.pallas_call(
        kern,
        grid=(b, t // rows),
        in_specs=[
            row_spec, row_spec, row_spec, row_spec,
            _const_spec((D_MODEL, D_MODEL)),
            _const_spec((1, D_MODEL)),
            _const_spec((D_MODEL, 2 * D_FF)),
            _const_spec((FFN_CONV, 2 * D_FF)),
            _const_spec((1, 2 * D_FF)),
            _const_spec((D_FF, D_MODEL)),
            _const_spec((1, D_MODEL)),
            _const_spec((SUBLANES, 2 * D_FF)),
        ],
        out_specs=(row_spec, pl.BlockSpec((None, SUBLANES, 2 * D_FF), lambda bi, i: (bi, 0, 0))),
        out_shape=(jax.ShapeDtypeStruct((b, t, D_MODEL), jnp.float32),
                   jax.ShapeDtypeStruct((b, SUBLANES, 2 * D_FF), jnp.float32)),
        scratch_shapes=[pltpu.VMEM((SUBLANES + rows, 2 * D_FF), jnp.float32)],
        compiler_params=pltpu.CompilerParams(
            dimension_semantics=("arbitrary", "arbitrary"), vmem_limit_bytes=VMEM_LIMIT),
        name="prompt_out",
    )(x, attn, ga, mr, pw["w_out"], pw["g_ffn"], pw["w_up"], pw["w_ffn_conv"], pw["b_ffn_conv"],
      pw["w_down"], pw["g_final"], tail0)


def _sample_in_kernel(x_ref, gmix_ref, win_ref, wq_ref, cw_ref, cb_ref, wg_ref, br_ref, bi_ref,
                      lam_ref, h0_ref, conv_ref,
                      q_ref, k_ref, v_ref, ga_ref, mr_ref, h_ref, convn_ref):
    n = _bf16(_rms_norm(x_ref[...], gmix_ref[...]))
    q_ref[...] = _bf16(_dot(n, wq_ref[...]) * SB_SCALE)
    k_ref[...] = _dot(n, win_ref[:, C_K:C_V])
    v_ref[...] = _dot(n, win_ref[:, C_V:C_XR])
    xr = _dot(n, win_ref[:, C_XR:C_GR])
    xc = cb_ref[...] + cw_ref[3:4, :] * xr
    for j in range(RNN_CONV - 1):
        xc = xc + cw_ref[j:j + 1, :] * conv_ref[j]
    for j in range(RNN_CONV - 2):
        convn_ref[j] = conv_ref[j + 1]
    convn_ref[RNN_CONV - 2] = xr
    a, u = _rglru_gates(xc, wg_ref, br_ref[...], bi_ref[...], lam_ref[...])
    h = a * h0_ref[...] + u
    h_ref[...] = h
    rnn = h * _gelu_tanh(_dot(n, win_ref[:, C_GR:C_GA]))
    ga_ref[...] = _sigmoid(_dot(n, win_ref[:, C_GA:C_GB]))
    mr_ref[...] = _sigmoid(_dot(n, win_ref[:, C_GB:C_END])) * rnn


def _sample_in(x, pw, h0, conv):
    db = x.shape[0]
    f32 = jnp.float32
    args = (x, pw["g_mix"], pw["w_in"], pw["w_qbd"], pw["w_rnn_conv"], pw["b_rnn_conv"],
            pw["w_gate"], pw["b_rgate"], pw["b_igate"], pw["lam"], h0, conv)
    out_shape = (
        jax.ShapeDtypeStruct((db, N_HEADS * D_KV), jnp.bfloat16),
        jax.ShapeDtypeStruct((db, D_KV), f32), jax.ShapeDtypeStruct((db, D_KV), f32),
        jax.ShapeDtypeStruct((db, D_MODEL), f32), jax.ShapeDtypeStruct((db, D_MODEL), f32),
        jax.ShapeDtypeStruct((db, D_RNN), f32),
        jax.ShapeDtypeStruct((RNN_CONV - 1, db, D_RNN), f32),
    )
    return pl.pallas_call(
        _sample_in_kernel,
        grid=(1,),
        in_specs=[_const_spec(a.shape) for a in args],
        out_specs=tuple(_whole_spec(s.shape) for s in out_shape),
        out_shape=out_shape,
        compiler_params=pltpu.CompilerParams(
            dimension_semantics=("arbitrary",), vmem_limit_bytes=VMEM_LIMIT),
        name="sample_in",
    )(*args)


PAGES_PER_CHUNK = 32


def _sample_attn_kernel(pt_ref, q_ref, bias_ref, kn_ref, vn_ref, ck_ref, cv_ref, o_ref,
                        kbuf, vbuf, sem, carry_s, acc_s, *, n_pages, past_len):
    b = pl.program_id(0)
    c = pl.program_id(1)
    nb = pl.num_programs(0)
    nch = n_pages // PAGES_PER_CHUNK
    step = b * nch + c
    slot = step % 2

    def copies(bb, cc, sl):
        first = (nch - 1 - cc) * PAGES_PER_CHUNK
        out = []
        for p in range(PAGES_PER_CHUNK):
            page = pt_ref[bb * n_pages + first + p]
            out.append(pltpu.make_async_copy(ck_ref.at[page], kbuf.at[sl, p], sem.at[0, sl]))
            out.append(pltpu.make_async_copy(cv_ref.at[page], vbuf.at[sl, p], sem.at[1, sl]))
        return out

    @pl.when(step == 0)
    def _():
        for cp in copies(b, c, slot):
            cp.start()

    @pl.when(step + 1 < nb * nch)
    def _():
        nxt = step + 1
        for cp in copies(nxt // nch, nxt % nch, 1 - slot):
            cp.start()

    for cp in copies(b, c, slot):
        cp.wait()

    @pl.when(c == 0)
    def _():
        carry_s[...] = jnp.zeros_like(carry_s)
        acc_s[...] = jnp.zeros_like(acc_s)

    q = q_ref[...]
    bias = bias_ref[...]
    later = _later_matrix(MXU_DIM)
    pages_per_tile = MXU_DIM // PAGE_SIZE
    carry = carry_s[...]
    acc = acc_s[...]
    for tile in reversed(range(PAGES_PER_CHUNK // pages_per_tile)):
        p0 = tile * pages_per_tile
        kt = _bf16(kbuf[slot, p0:p0 + pages_per_tile].reshape(MXU_DIM, D_KV))
        vt = _bf16(vbuf[slot, p0:p0 + pages_per_tile].reshape(MXU_DIM, D_KV))
        z = lax.dot_general(q, kt, (((1,), (1,)), ((), ())),
                            preferred_element_type=jnp.float32) + bias
        lsz, lk = _log_sigmoid_pair(z)
        tail = _dot(_bf16(lk), later) + carry
        w = jnp.exp(lsz + tail)
        acc = acc + _dot(_bf16(w), vt)
        carry = carry + jnp.sum(lk, axis=-1, keepdims=True)
    carry_s[...] = carry
    acc_s[...] = acc

    @pl.when(c == nch - 1)
    def _():
        k_pos = past_len + lax.broadcasted_iota(jnp.int32, (N_HEADS, 1), 1)
        self_mask = k_pos < past_len
        z_self = jnp.sum(q.astype(jnp.float32) * _bf16(kn_ref[...]).astype(jnp.float32),
                         axis=-1, keepdims=True) + bias
        ls_self, _ = _log_sigmoid_pair(z_self)
        w_self = jnp.where(self_mask, jnp.exp(ls_self), 0.0)
        full = acc + w_self * _bf16(vn_ref[...]).astype(jnp.float32)
        row_head = lax.broadcasted_iota(jnp.int32, full.shape, 0) % N_KV_HEADS
        lane_head = lax.broadcasted_iota(jnp.int32, full.shape, 1) // HEAD_DIM
        full = jnp.where(row_head == lane_head, full, 0.0)
        out = full[:, 0:HEAD_DIM]
        for h in range(1, N_KV_HEADS):
            out = out + full[:, h * HEAD_DIM:(h + 1) * HEAD_DIM]
        o_ref[...] = out


def _sample_attn(page_table, qbd, bias_col, k_new, v_new, cache_k, cache_v):
    db, n_pages = page_table.shape
    nch = n_pages // PAGES_PER_CHUNK
    past_len = n_pages * PAGE_SIZE
    kern = functools.partial(_sample_attn_kernel, n_pages=n_pages, past_len=past_len)
    grid_spec = pltpu.PrefetchScalarGridSpec(
        num_scalar_prefetch=1,
        grid=(db, nch),
        in_specs=[
            pl.BlockSpec((None, N_HEADS, D_KV), lambda b, c, pt: (b, 0, 0)),
            pl.BlockSpec((N_HEADS, 1), lambda b, c, pt: (0, 0)),
            pl.BlockSpec((None, 1, D_KV), lambda b, c, pt: (b, 0, 0)),
            pl.BlockSpec((None, 1, D_KV), lambda b, c, pt: (b, 0, 0)),
            pl.BlockSpec(memory_space=pl.ANY),
            pl.BlockSpec(memory_space=pl.ANY),
        ],
        out_specs=pl.BlockSpec((None, N_HEADS, HEAD_DIM), lambda b, c, pt: (b, 0, 0)),
        scratch_shapes=[
            pltpu.VMEM((2, PAGES_PER_CHUNK, PAGE_SIZE, D_KV), jnp.float32),
            pltpu.VMEM((2, PAGES_PER_CHUNK, PAGE_SIZE, D_KV), jnp.float32),
            pltpu.SemaphoreType.DMA((2, 2)),
            pltpu.VMEM((N_HEADS, 1), jnp.float32),
            pltpu.VMEM((N_HEADS, D_KV), jnp.float32),
        ],
    )
    return pl.pallas_call(
        kern,
        grid_spec=grid_spec,
        out_shape=jax.ShapeDtypeStruct((db, N_HEADS, HEAD_DIM), jnp.float32),
        compiler_params=pltpu.CompilerParams(
            dimension_semantics=("arbitrary", "arbitrary"), vmem_limit_bytes=VMEM_LIMIT),
        name="sample_attn",
    )(page_table.reshape(-1), qbd, bias_col, k_new, v_new, cache_k, cache_v)


def _sample_out_kernel(x_ref, at_ref, ga_ref, mr_ref, wout_ref, gffn_ref, wup_ref, cw_ref, cb_ref,
                       wdown_ref, gfin_ref, st_ref, y_ref, stn_ref):
    merged = _bf16(ga_ref[...] * at_ref[...] + mr_ref[...])
    x1 = x_ref[...] + _dot(merged, wout_ref[...])
    n2 = _bf16(_rms_norm(x1, gffn_ref[...]))

    def up_fn(c0, c1):
        pre = _dot(n2, wup_ref[:, c0:c1])
        out = cb_ref[:, c0:c1] + cw_ref[2:3, c0:c1] * pre
        for j in range(FFN_CONV - 1):
            out = out + cw_ref[j:j + 1, c0:c1] * st_ref[j, :, c0:c1]
        for j in range(FFN_CONV - 2):
            stn_ref[j, :, c0:c1] = st_ref[j + 1, :, c0:c1]
        stn_ref[FFN_CONV - 2, :, c0:c1] = pre
        return out

    y_ref[...] = _ffn_tail(x1, n2, up_fn, wdown_ref, gfin_ref)


def _sample_out(x, attn, ga, mr, pw, state):
    db = x.shape[0]
    args = (x, attn, ga, mr, pw["w_out"], pw["g_ffn"], pw["w_up"], pw["w_ffn_conv"],
            pw["b_ffn_conv"], pw["w_down"], pw["g_final"], state)
    out_shape = (jax.ShapeDtypeStruct((db, D_MODEL), jnp.float32),
                 jax.ShapeDtypeStruct((FFN_CONV - 1, db, 2 * D_FF), jnp.float32))
    return pl.pallas_call(
        _sample_out_kernel,
        grid=(1,),
        in_specs=[_const_spec(a.shape) for a in args],
        out_specs=tuple(_whole_spec(s.shape) for s in out_shape),
        out_shape=out_shape,
        compiler_params=pltpu.CompilerParams(
            dimension_semantics=("arbitrary",), vmem_limit_bytes=VMEM_LIMIT),
        name="sample_out",
    )(*args)


def _prepare_weights(g_mix, w_in, sb_bias, w_rnn_conv, b_rnn_conv, w_rgate, b_rgate, w_igate,
                     b_igate, rglru_lambda, w_out, g_ffn, w_up, w_ffn_conv, b_ffn_conv, w_down,
                     g_final):
    perm = jnp.asarray(_PERM)
    w_q, w_k, w_v, w_xr, w_gr, w_ga, w_gb = (
        w_in[:, 0:1024], w_in[:, 1024:1280], w_in[:, 1280:1536], w_in[:, 1536:2560],
        w_in[:, 2560:3584], w_in[:, 3584:4608], w_in[:, 4608:5632])
    w_in_r = jnp.concatenate(
        [w_q[:, perm], w_k, w_v, w_xr[:, perm], w_gr[:, perm], w_ga[:, perm], w_gb[:, perm]],
        axis=1).astype(jnp.bfloat16)

    wq_heads = w_q[:, perm].reshape(D_MODEL, N_HEADS, 1, HEAD_DIM)
    kv_of_head = np.arange(N_HEADS) % N_KV_HEADS
    sel = jnp.asarray(kv_of_head[:, None] == np.arange(N_KV_HEADS)[None, :])
    w_qbd = jnp.where(sel[None, :, :, None], wq_heads, 0.0).reshape(D_MODEL, N_HEADS * D_KV)

    def tiles(wblocks):
        wb = wblocks[jnp.asarray(_BLOCK_PERM)]
        wb = wb.reshape(D_RNN // MXU_DIM, MXU_DIM // RNN_BLOCK, RNN_BLOCK, RNN_BLOCK)
        eye = jnp.eye(MXU_DIM // RNN_BLOCK, dtype=wb.dtype)
        t = jnp.einsum("mncd,nk->mnckd", wb, eye)
        return t.reshape(D_RNN // MXU_DIM, MXU_DIM, MXU_DIM)

    w_gate = jnp.concatenate([tiles(w_rgate), tiles(w_igate)], axis=-1).astype(jnp.bfloat16)

    row = lambda a: a.reshape(1, -1)
    return {
        "g_mix": row(g_mix), "w_in": w_in_r, "w_qbd": w_qbd.astype(jnp.bfloat16),
        "w_rnn_conv": w_rnn_conv[:, perm], "b_rnn_conv": row(b_rnn_conv[perm]),
        "w_gate": w_gate, "b_rgate": row(b_rgate[perm]), "b_igate": row(b_igate[perm]),
        "lam": row(rglru_lambda[perm]),
        "w_out": w_out[perm, :].astype(jnp.bfloat16), "g_ffn": row(g_ffn),
        "w_up": w_up.astype(jnp.bfloat16), "w_ffn_conv": w_ffn_conv, "b_ffn_conv": row(b_ffn_conv),
        "w_down": w_down.astype(jnp.bfloat16), "g_final": row(g_final),
        "sb_bias": sb_bias,
    }


def kernel(x_prompt, x_sample, cache_k, cache_v, page_table, state_rnn_h, state_rnn_conv,
           state_ffn_conv, meta_tokens, g_mix, w_in, sb_bias, w_rnn_conv, b_rnn_conv, w_rgate,
           b_rgate, w_igate, b_igate, rglru_lambda, w_out, g_ffn, w_up, w_ffn_conv, b_ffn_conv,
           w_down, g_final):
    assert w_in.shape[0] == 1, "single-layer model"
    b, seq, _ = x_prompt.shape
    db = x_sample.shape[0]
    perm = jnp.asarray(_PERM)
    inv_perm = jnp.asarray(_INV_PERM)
    pw = _prepare_weights(g_mix[0], w_in[0], sb_bias[0], w_rnn_conv[0], b_rnn_conv[0], w_rgate[0],
                          b_rgate[0], w_igate[0], b_igate[0], rglru_lambda[0], w_out[0], g_ffn[0],
                          w_up[0], w_ffn_conv[0], b_ffn_conv[0], w_down[0], g_final)
    f32 = jnp.float32

    pad = META_BLOCK - META_TOKENS
    x_meta = jnp.pad(meta_tokens.astype(f32), ((pad, 0), (0, 0)))[None]
    zero_h = jnp.zeros((1, D_RNN), f32)
    zero_conv = jnp.zeros((SUBLANES, D_RNN), f32)
    (qm, km, vm, ktm, vbm, gam, mrm, hm, convm) = _prompt_in(
        x_meta, pw, zero_h, zero_conv, rows=META_BLOCK, tk=META_BLOCK, first_valid=pad)
    ktm = ktm[0]
    vbm = vbm[0, 0]
    attn_m = _prompt_attn(pw["sb_bias"], qm, ktm[None], vbm[None, None], ktm, vbm,
                          tq=META_BLOCK, main=False)
    zero_tail = jnp.zeros((SUBLANES, 2 * D_FF), f32)
    _, tail_m = _prompt_out(x_meta, attn_m, gam, mrm, pw, zero_tail, rows=META_BLOCK)

    (qp, kp, vp, ktp, vbp, gap, mrp, hp, convp) = _prompt_in(
        x_prompt, pw, hm[0], convm[0], rows=256, tk=MXU_DIM, first_valid=0)
    attn_p = _prompt_attn(pw["sb_bias"], qp, ktp, vbp, ktm, vbm, tq=MXU_DIM, main=True)
    y_prompt, tail_p = _prompt_out(x_prompt, attn_p, gap, mrp, pw, tail_m[0], rows=256)

    k_meta = jnp.broadcast_to(km[:, pad:], (b, META_TOKENS, D_KV))
    v_meta = jnp.broadcast_to(vm[:, pad:], (b, META_TOKENS, D_KV))
    prompt_k = jnp.concatenate([k_meta, kp], axis=1).reshape(1, b, META_TOKENS + seq, N_KV_HEADS, HEAD_DIM)
    prompt_v = jnp.concatenate([v_meta, vp], axis=1).reshape(1, b, META_TOKENS + seq, N_KV_HEADS, HEAD_DIM)
    prompt_h = hp[:, 0, :][:, inv_perm][None]
    prompt_rnn_conv = convp[:, SUBLANES - (RNN_CONV - 1):, :][:, :, inv_perm][None]
    prompt_ffn_conv = tail_p[:, SUBLANES - (FFN_CONV - 1):, :][None]

    xs = x_sample[:, 0, :]
    h0 = state_rnn_h[0][:, perm]
    conv_s = jnp.transpose(state_rnn_conv[0], (1, 0, 2))[:, :, perm]
    (qs, ks, vs, gas, mrs, hs, convs) = _sample_in(xs, pw, h0, conv_s)
    n_phys = cache_k.shape[1]
    bias_col = pw["sb_bias"][perm.reshape(N_HEADS, HEAD_DIM)[:, 0] // HEAD_DIM].reshape(N_HEADS, 1)
    attn_s = _sample_attn(page_table, qs.reshape(db, N_HEADS, D_KV), bias_col,
                          ks.reshape(db, 1, D_KV), vs.reshape(db, 1, D_KV),
                          cache_k[0].reshape(n_phys, PAGE_SIZE, D_KV),
                          cache_v[0].reshape(n_phys, PAGE_SIZE, D_KV))
    ffn_s = jnp.transpose(state_ffn_conv[0], (1, 0, 2))
    y_s, ffn_new = _sample_out(xs, attn_s.reshape(db, D_MODEL), gas, mrs, pw, ffn_s)

    y_sample = y_s[:, None, :]
    sample_k = ks.reshape(1, db, 1, N_KV_HEADS, HEAD_DIM)
    sample_v = vs.reshape(1, db, 1, N_KV_HEADS, HEAD_DIM)
    sample_h = hs[:, inv_perm][None]
    sample_rnn_conv = jnp.transpose(convs, (1, 0, 2))[:, :, inv_perm][None]
    sample_ffn_conv = jnp.transpose(ffn_new, (1, 0, 2))[None]

    return (y_prompt, y_sample, prompt_k, prompt_v, prompt_h, prompt_rnn_conv, prompt_ffn_conv,
            sample_k, sample_v, sample_h, sample_rnn_conv, sample_ffn_conv)
```
